```python
import math
import jax
import jax.numpy as jnp
from jax import lax
import numpy as np

D_MODEL = 1024
BATCH = 1
SEQ = 16384
DEPTH = 4
DEC_BATCH = 32
DEC_SEQ = 64
PAST_LEN = 4096

CHUNK = 64
EPS = 1e-6
N_BRANCH = 3
D_FF = 4 * D_MODEL

A_HEADS = 4
A_DK = 128
A_DV = 128
A_KW = A_HEADS * A_DK
A_VW = A_HEADS * A_DV
A_BLOCK = 32

B_WIDTH = 512
B_BLOCKS = 8
B_BW = B_WIDTH // B_BLOCKS
B_CONV = 4
RG_C = 8.0

C_HEADS = 4
C_DK = 128
C_DV = 128
C_KW = C_HEADS * C_DK
C_VW = C_HEADS * C_DV
C_QKV = 2 * C_KW + C_VW
C_CONV = 4
C_BLOCK = CHUNK

IN_SIZES = (A_KW, A_KW, A_VW, A_VW, B_WIDTH, B_WIDTH, C_QKV, C_VW, C_HEADS, C_HEADS, N_BRANCH * D_MODEL)
D_IN = 2 * A_KW + 2 * A_VW + 2 * B_WIDTH + C_QKV + C_VW + 2 * C_HEADS + N_BRANCH * D_MODEL

kernel_name = 'hybrid_stream_hgrn2_rglru_gdn_step'


def split_points():
    pts, acc = [], 0
    for s in IN_SIZES[:-1]:
        acc += s
        pts.append(acc)
    return pts


def rms_norm(x, w):
    xf = x.astype(jnp.float32)
    y = xf * lax.rsqrt(jnp.mean(xf * xf, axis=-1, keepdims=True) + EPS)
    return (y * w.astype(jnp.float32)).astype(x.dtype)


def l2_norm(x):
    return x * lax.rsqrt(jnp.sum(x * x, axis=-1, keepdims=True) + EPS)


def causal_dwconv(x, prev, w):
    width = w.shape[0]
    t = x.shape[1]
    xp = jnp.concatenate([prev.astype(x.dtype), x], axis=1)
    y = xp[:, 0:t] * w[0]
    for j in range(1, width):
        y = y + xp[:, j:j + t] * w[j]
    return y, xp[:, t:]


def to_blocks(a, block):
    b, t = a.shape[:2]
    n = -(-t // block)
    a = jnp.pad(a, [(0, 0), (0, n * block - t)] + [(0, 0)] * (a.ndim - 2))
    a = a.reshape((b, n, block) + a.shape[2:])
    perm = (1, 0, 3, 2) + tuple(range(4, a.ndim))
    return a.transpose(perm)


def from_blocks(o, t):
    n, b, h, l, d = o.shape
    return o.transpose(1, 0, 3, 2, 4).reshape(b, n * l, h, d)[:, :t]


def hgrn2_core(q, k, v, logf, s0):
    t = q.shape[1]
    qb, kb, vb, gb = (to_blocks(a, A_BLOCK) for a in (q, k, v, logf))
    incl = jnp.tril(jnp.ones((A_BLOCK, A_BLOCK), dtype=bool))

    def step(s, inp):
        q_i, k_i, v_i, g_i = inp
        b = jnp.cumsum(g_i, axis=2)
        diff = b[:, :, :, None, :] - b[:, :, None, :, :]
        decay = jnp.exp(jnp.where(incl[:, :, None], diff, -jnp.inf))
        attn = jnp.einsum('bhtk,bhsk,bhtsk->bhts', q_i, k_i, decay)
        o = jnp.einsum('bhts,bhsv->bhtv', attn, v_i) + jnp.einsum('bhtk,bhkv->bhtv', q_i * jnp.exp(b), s)
        b_last = b[:, :, -1:, :]
        s = jnp.exp(b_last)[:, :, 0, :, None] * s + jnp.einsum('bhsk,bhsv->bhkv', k_i * jnp.exp(b_last - b), v_i)
        return s, o

    s, o = lax.scan(step, s0.astype(jnp.float32), (qb, kb, vb, gb))
    return from_blocks(o, t), s


def gdn_core(q, k, v, g, beta, s0):
    t = q.shape[1]
    qb, kb, vb = (to_blocks(a, C_BLOCK) for a in (q, k, v))
    gb, bb = to_blocks(g, C_BLOCK), to_blocks(beta, C_BLOCK)
    gc = jnp.cumsum(gb, axis=-1)
    incl = jnp.tril(jnp.ones((C_BLOCK, C_BLOCK), dtype=bool))
    strict = jnp.tril(jnp.ones((C_BLOCK, C_BLOCK), dtype=bool), -1)
    decay = jnp.exp(jnp.where(incl, gc[..., :, None] - gc[..., None, :], -jnp.inf))
    k_beta = kb * bb[..., None]
    a_mat = jnp.where(strict, jnp.einsum('nbhtk,nbhsk->nbhts', k_beta, kb) * decay, 0.0)
    eye = jnp.eye(C_BLOCK, dtype=jnp.float32)
    t_mat = lax.linalg.triangular_solve(eye + a_mat, jnp.broadcast_to(eye, a_mat.shape),
                                        left_side=True, lower=True, unit_diagonal=True)
    u = jnp.einsum('nbhts,nbhsv->nbhtv', t_mat, vb * bb[..., None])
    w = jnp.einsum('nbhts,nbhsk->nbhtk', t_mat, k_beta * jnp.exp(gc)[..., None])
    qk = jnp.where(incl, jnp.einsum('nbhtk,nbhsk->nbhts', qb, kb) * decay, 0.0)

    def step(s, inp):
        q_i, k_i, u_i, w_i, g_i, qk_i = inp
        v_new = u_i - jnp.einsum('bhtk,bhkv->bhtv', w_i, s)
        o = (jnp.einsum('bhtk,bhkv->bhtv', q_i * jnp.exp(g_i)[..., None], s)
             + jnp.einsum('bhts,bhsv->bhtv', qk_i, v_new))
        g_last = g_i[..., -1:]
        s = (s * jnp.exp(g_last)[..., None]
             + jnp.einsum('bhtk,bhtv->bhkv', k_i * jnp.exp(g_last - g_i)[..., None], v_new))
        return s, o

    s, o = lax.scan(step, s0.astype(jnp.float32), (qb, kb, u, w, gc, qk))
    return from_blocks(o, t), s


def rg_lru(x, h0, w_a, b_a, w_x, b_x, lam, first):
    f32 = jnp.float32
    b, t, _ = x.shape
    xf = x.astype(f32)
    xb = xf.reshape(b, t, B_BLOCKS, B_BW)
    r = jax.nn.sigmoid(jnp.einsum('btnc,ncd->btnd', xb, w_a.astype(f32)).reshape(b, t, B_WIDTH) + b_a.astype(f32))
    i = jax.nn.sigmoid(jnp.einsum('btnc,ncd->btnd', xb, w_x.astype(f32)).reshape(b, t, B_WIDTH) + b_x.astype(f32))
    log_a = -RG_C * r * jax.nn.softplus(-lam.astype(f32))
    a = jnp.exp(log_a)
    mult = jnp.sqrt(-jnp.expm1(2.0 * log_a))
    if first:
        mult = mult.at[:, 0].set(1.0)
    u = mult * i * xf
    u = u.at[:, 0].add(a[:, 0] * h0.astype(f32))

    def combine(e1, e2):
        return e1[0] * e2[0], e2[0] * e1[1] + e2[1]

    _, h = lax.associative_scan(combine, (a, u), axis=1)
    return h, h[:, -1]


def trunk_layer(x, states, p, lb, first):
    f32 = jnp.float32
    s_a, h_b, conv_b, s_c, conv_c = states
    bsz, t, _ = x.shape
    h = rms_norm(x, p['pre_mix'])
    z = h @ p['w_in']
    (za_q, za_f, za_i, za_g, zb_x, zb_g, zc_qkv, zc_g, zc_beta, zc_alpha, z_merge) = jnp.split(z, split_points(), axis=-1)

    qa = jax.nn.silu(za_q.astype(f32)).reshape(bsz, t, A_HEADS, A_DK)
    zf = za_f.astype(f32)
    lbf = lb.astype(f32)
    logf = jnp.logaddexp(jnp.log(lbf), jnp.log1p(-lbf) + jax.nn.log_sigmoid(zf))
    ka = (1.0 - lbf) * jax.nn.sigmoid(-zf)
    va = za_i.astype(f32).reshape(bsz, t, A_HEADS, A_DV)
    oa, s_a_new = hgrn2_core(qa, ka.reshape(bsz, t, A_HEADS, A_DK), va, logf.reshape(bsz, t, A_HEADS, A_DK), s_a)
    oa = rms_norm(oa, p['a_norm']) * jax.nn.silu(za_g.astype(f32)).reshape(bsz, t, A_HEADS, A_DV)
    oa = oa.reshape(bsz, t, A_VW).astype(x.dtype)

    xb_c, conv_b_new = causal_dwconv(zb_x, conv_b, p['b_conv_w'])
    xb_c = xb_c + p['b_conv_b']
    hb, h_b_new = rg_lru(xb_c, h_b, p['b_gate_a_w'], p['b_gate_a_b'], p['b_gate_x_w'], p['b_gate_x_b'], p['b_lambda'], first)
    ob = (jax.nn.gelu(zb_g.astype(f32)) * hb).astype(x.dtype)

    qkv, conv_c_new = causal_dwconv(zc_qkv, conv_c, p['c_conv_w'])
    qkv = jax.nn.silu(qkv.astype(f32))
    qc, kc, vc = jnp.split(qkv, [C_KW, 2 * C_KW], axis=-1)
    qc = l2_norm(qc.reshape(bsz, t, C_HEADS, C_DK)) * (C_DK ** -0.5)
    kc = l2_norm(kc.reshape(bsz, t, C_HEADS, C_DK))
    vc = vc.reshape(bsz, t, C_HEADS, C_DV)
    beta = jax.nn.sigmoid(zc_beta.astype(f32))
    gdec = -jnp.exp(p['c_a_log'].astype(f32)) * jax.nn.softplus(zc_alpha.astype(f32) + p['c_dt_bias'].astype(f32))
    oc, s_c_new = gdn_core(qc, kc, vc, gdec, beta, s_c)
    oc = rms_norm(oc, p['c_norm']) * jax.nn.silu(zc_g.astype(f32)).reshape(bsz, t, C_HEADS, C_DV)
    oc = oc.reshape(bsz, t, C_VW).astype(x.dtype)

    gates = jax.nn.sigmoid(z_merge.astype(f32)).reshape(bsz, t, N_BRANCH, D_MODEL).astype(x.dtype)
    merged = (gates[:, :, 0] * (oa @ p['w_br_a']) + gates[:, :, 1] * (ob @ p['w_br_b'])
              + gates[:, :, 2] * (oc @ p['w_br_c']))
    x = x + rms_norm(merged @ p['w_out'], p['post_mix'])

    h2 = rms_norm(x, p['pre_mlp'])
    m = jnp.square(jax.nn.relu(h2 @ p['w_up'])) @ p['w_down']
    x = x + rms_norm(m, p['post_mlp'])
    new = (s_a_new, h_b_new, conv_b_new, s_c_new, conv_c_new)
    return x, tuple(n.astype(o.dtype) for n, o in zip(new, states))


def setup_inputs(seed: int = 0) -> dict:
    key = jax.random.key(seed)
    ks = jax.random.split(key, 32)
    f32 = jnp.float32

    def nrm(k, shape, scale):
        return scale * jax.random.normal(k, shape, f32)

    def gain(k, shape):
        return 1.0 + 0.02 * jax.random.normal(k, shape, f32)

    u = jax.random.uniform(ks[20], (DEPTH, B_WIDTH), f32, 0.9, 0.999)
    a_base = u ** (1.0 / RG_C)
    b_lambda = jnp.log(a_base) - jnp.log1p(-a_base)
    c_a_log = jnp.log(jax.random.uniform(ks[22], (DEPTH, C_HEADS), f32, 1.0, 16.0))
    dt = jnp.exp(jax.random.uniform(ks[23], (DEPTH, C_HEADS), f32, math.log(1e-3), math.log(1e-1)))
    c_dt_bias = dt + jnp.log(-jnp.expm1(-dt))
    return {
        'x_prompt': nrm(ks[0], (BATCH, SEQ, D_MODEL), 1.0),
        'x_sample': nrm(ks[1], (DEC_BATCH, DEC_SEQ, D_MODEL), 1.0),
        'state_hgrn': nrm(ks[2], (DEPTH, DEC_BATCH, A_HEADS, A_DK, A_DV), 0.5),
        'state_rglru': nrm(ks[3], (DEPTH, DEC_BATCH, B_WIDTH), 0.5),
        'state_rglru_conv': nrm(ks[4], (DEPTH, DEC_BATCH, B_CONV - 1, B_WIDTH), 1.0),
        'state_gdn': nrm(ks[5], (DEPTH, DEC_BATCH, C_HEADS, C_DK, C_DV), 0.1),
        'state_gdn_conv': nrm(ks[6], (DEPTH, DEC_BATCH, C_CONV - 1, C_QKV), 1.0),
        'lb_raw': nrm(ks[7], (DEPTH, A_KW), 0.1),
        'norm_pre_mix': gain(ks[8], (DEPTH, D_MODEL)),
        'norm_post_mix': gain(ks[9], (DEPTH, D_MODEL)),
        'norm_pre_mlp': gain(ks[10], (DEPTH, D_MODEL)),
        'norm_post_mlp': gain(ks[11], (DEPTH, D_MODEL)),
        'w_in': nrm(ks[12], (DEPTH, D_MODEL, D_IN), D_MODEL ** -0.5),
        'a_norm': gain(ks[13], (DEPTH, A_DV)),
        'b_conv_w': nrm(ks[14], (DEPTH, B_CONV, B_WIDTH), 0.5),
        'b_conv_b': nrm(ks[15], (DEPTH, B_WIDTH), 0.01),
        'b_gate_a_w': nrm(ks[16], (DEPTH, B_BLOCKS, B_BW, B_BW), B_BW ** -0.5),
        'b_gate_a_b': nrm(ks[17], (DEPTH, B_WIDTH), 0.01),
        'b_gate_x_w': nrm(ks[18], (DEPTH, B_BLOCKS, B_BW, B_BW), B_BW ** -0.5),
        'b_gate_x_b': nrm(ks[19], (DEPTH, B_WIDTH), 0.01),
        'b_lambda': b_lambda,
        'c_conv_w': nrm(ks[21], (DEPTH, C_CONV, C_QKV), 0.5),
        'c_a_log': c_a_log,
        'c_dt_bias': c_dt_bias,
        'c_norm': gain(ks[24], (DEPTH, C_DV)),
        'w_br_a': nrm(ks[25], (DEPTH, A_VW, D_MODEL), A_VW ** -0.5),
        'w_br_b': nrm(ks[26], (DEPTH, B_WIDTH, D_MODEL), B_WIDTH ** -0.5),
        'w_br_c': nrm(ks[27], (DEPTH, C_VW, D_MODEL), C_VW ** -0.5),
        'w_out': nrm(ks[28], (DEPTH, D_MODEL, D_MODEL), D_MODEL ** -0.5),
        'w_up': nrm(ks[29], (DEPTH, D_MODEL, D_FF), D_MODEL ** -0.5),
        'w_down': nrm(ks[30], (DEPTH, D_FF, D_MODEL), (1.5 * D_FF) ** -0.5),
    }


def reference(x_prompt, x_sample, state_hgrn, state_rglru, state_rglru_conv, state_gdn, state_gdn_conv,
              lb_raw, norm_pre_mix, norm_post_mix, norm_pre_mlp, norm_post_mlp, w_in, a_norm,
              b_conv_w, b_conv_b, b_gate_a_w, b_gate_a_b, b_gate_x_w, b_gate_x_b, b_lambda,
              c_conv_w, c_a_log, c_dt_bias, c_norm, w_br_a, w_br_b, w_br_c, w_out, w_up, w_down):
    f32 = jnp.float32
    lb_cum = jnp.cumsum(jax.nn.softmax(lb_raw.astype(f32), axis=0), axis=0)
    lower_bounds = lb_cum - lb_cum[0:1]
    bp = x_prompt.shape[0]
    sdt = state_hgrn.dtype
    prompt_state = (jnp.zeros((bp, A_HEADS, A_DK, A_DV), sdt), jnp.zeros((bp, B_WIDTH), sdt),
                    jnp.zeros((bp, B_CONV - 1, B_WIDTH), sdt), jnp.zeros((bp, C_HEADS, C_DK, C_DV), sdt),
                    jnp.zeros((bp, C_CONV - 1, C_QKV), sdt))
    yp, ys = x_prompt, x_sample
    new_p, new_s = [], []
    for l in range(DEPTH):
        p = dict(pre_mix=norm_pre_mix[l], post_mix=norm_post_mix[l], pre_mlp=norm_pre_mlp[l],
                 post_mlp=norm_post_mlp[l], w_in=w_in[l], a_norm=a_norm[l], b_conv_w=b_conv_w[l],
                 b_conv_b=b_conv_b[l], b_gate_a_w=b_gate_a_w[l], b_gate_a_b=b_gate_a_b[l],
                 b_gate_x_w=b_gate_x_w[l], b_gate_x_b=b_gate_x_b[l], b_lambda=b_lambda[l],
                 c_conv_w=c_conv_w[l], c_a_log=c_a_log[l], c_dt_bias=c_dt_bias[l], c_norm=c_norm[l],
                 w_br_a=w_br_a[l], w_br_b=w_br_b[l], w_br_c=w_br_c[l], w_out=w_out[l],
                 w_up=w_up[l], w_down=w_down[l])
        yp, st_p = trunk_layer(yp, prompt_state, p, lower_bounds[l], True)
        new_p.append(st_p)
        st_in = (state_hgrn[l], state_rglru[l], state_rglru_conv[l], state_gdn[l], state_gdn_conv[l])
        ys, st_s = trunk_layer(ys, st_in, p, lower_bounds[l], False)
        new_s.append(st_s)
    p_hgrn, p_rglru, p_rglru_conv, p_gdn, p_gdn_conv = [jnp.stack([st[j] for st in new_p]) for j in range(5)]
    s_hgrn, s_rglru, s_rglru_conv, s_gdn, s_gdn_conv = [jnp.stack([st[j] for st in new_s]) for j in range(5)]
    return (yp, ys, p_hgrn, p_rglru, p_rglru_conv, p_gdn, p_gdn_conv,
            s_hgrn, s_rglru, s_rglru_conv, s_gdn, s_gdn_conv)
```

```python
import functools
import math

import jax
import jax.numpy as jnp
from jax import lax
from jax.experimental import pallas as pl
from jax.experimental.pallas import tpu as pltpu

F32 = jnp.float32
BF16 = jnp.bfloat16

D_MODEL = 1024
N_HEADS = 4
D_HEAD = 128
HW = N_HEADS * D_HEAD
B_WIDTH = 512
QKV = 3 * HW
D_FF = 4 * D_MODEL
CONV_W = 4
RG_C = 8.0
EPS = 1e-6
CHUNK = 64
SUB = 32

Z_A = 0
Z_B = 2048
Z_C = 3072
Z_BA = 5120
Z_MERGE = 5248
Z_TOT = 8320
Z_GROUPS = ((0, 2048), (2048, 3072), (3072, 5248), (5248, 8320))

MIX_TILE = 256
MLP_TILE = 512
VMEM_LIMIT = 60 * 1024 * 1024


def _dot(a, b):
    return jnp.dot(a, b, preferred_element_type=F32)


def _dot_nt(a, b):
    return lax.dot_general(a, b, (((1,), (1,)), ((), ())), preferred_element_type=F32)


def _dot_tn(a, b):
    return lax.dot_general(a, b, (((0,), (0,)), ((), ())), preferred_element_type=F32)


def _sigmoid(x):
    return 1.0 / (1.0 + jnp.exp(-x))


def _silu(x):
    return x * _sigmoid(x)


def _softplus(x):
    return jnp.maximum(x, 0.0) + jnp.log1p(jnp.exp(-jnp.abs(x)))


def _log_sigmoid(x):
    return jnp.minimum(x, 0.0) - jnp.log1p(jnp.exp(-jnp.abs(x)))


def _logaddexp(a, b):
    return jnp.maximum(a, b) + jnp.log1p(jnp.exp(-jnp.abs(a - b)))


def _gelu_tanh(x):
    return 0.5 * x * (1.0 + jnp.tanh(math.sqrt(2.0 / math.pi) * (x + 0.044715 * (x * x * x))))


def _rms(x, w):
    return x * lax.rsqrt(jnp.mean(x * x, axis=-1, keepdims=True) + EPS) * w


def _split3(x):
    hi = x.astype(BF16)
    r = x - hi.astype(F32)
    mid = r.astype(BF16)
    lo = (r - mid.astype(F32)).astype(BF16)
    return hi, mid, lo


def _cumsum_rows(tri, x):
    hi, mid, lo = _split3(x)
    return _dot(tri, hi) + _dot(tri, mid) + _dot(tri, lo)


def _cumsum_cols(x, triu):
    hi, mid, lo = _split3(x)
    return _dot(hi, triu) + _dot(mid, triu) + _dot(lo, triu)


def _unit_lower_inverse(a, rowid, colid):
    n = a.shape[0]
    td = jnp.where(rowid == colid, 1.0, 0.0).astype(F32)
    shift = 0
    while (1 << shift) < n:
        rb = rowid >> shift
        sib = ((rb & 1) == 1) & ((colid >> shift) == rb - 1)
        a_off = jnp.where(sib, a, 0.0)
        x = _dot(td.astype(BF16), a_off.astype(BF16))
        td = td - _dot(x.astype(BF16), td.astype(BF16))
        shift += 1
    return td


def _linear_scan(a, u, rowid):
    n = a.shape[0]
    d = 1
    while d < n:
        a_s = pltpu.roll(a, d, 0)
        u_s = pltpu.roll(u, d, 0)
        m = rowid >= d
        u = jnp.where(m, u + a * u_s, u)
        a = jnp.where(m, a * a_s, a)
        d *= 2
    return a, u


def _causal_conv(buf_ref, x, w):
    n = x.shape[0]
    buf_ref[8:8 + n, :] = x
    y = x * w[3:4, :]
    for j in range(CONV_W - 1):
        y = y + buf_ref[5 + j:5 + j + n, :] * w[j:j + 1, :]
    buf_ref[5:8, :] = x[n - 3:n, :]
    return y


def _mixer_kernel(n_ptiles,
                  x_ref, sh_ref, sr_ref, src_ref, sg_ref, sgc_ref,
                  win_ref, wbr_ref, wout_ref, wg_ref,
                  npre_ref, npost_ref, anorm_ref, cnorm_ref, lb_ref, bcw_ref, bcb_ref, gb_ref,
                  lam_ref, ccw_ref, alog_ref, dtb_ref,
                  xo_ref, ph_ref, pr_ref, prc_ref, pg_ref, pgc_ref,
                  oh_ref, or_ref, orc_ref, og_ref, ogc_ref,
                  z_s, o_s, sa_s, sc_s, hb_s, cvb_s, cvc_s):
    i = pl.program_id(0)
    tile = x_ref.shape[0]
    is_sample = i >= n_ptiles

    @pl.when(i == 0)
    def _():
        sa_s[...] = jnp.zeros_like(sa_s)
        sc_s[...] = jnp.zeros_like(sc_s)
        hb_s[...] = jnp.zeros_like(hb_s)
        cvb_s[0:8, :] = jnp.zeros((8, B_WIDTH), F32)
        cvc_s[0:8, :] = jnp.zeros((8, QKV), F32)

    x = x_ref[...]
    h = _rms(x, npre_ref[...]).astype(BF16)
    for c0, c1 in Z_GROUPS:
        z_s[:, c0:c1] = _dot(h, win_ref[:, c0:c1])

    lb = lb_ref[...]
    log_lb = jnp.log(lb)
    log_1m_lb = jnp.log1p(-lb)
    one_m_lb = 1.0 - lb
    sp_lam = _softplus(-lam_ref[...])
    neg_a = -jnp.exp(alog_ref[...])
    dtb = dtb_ref[...]
    a_norm = anorm_ref[...]
    c_norm = cnorm_ref[...]
    bcw = bcw_ref[...]
    bcb = bcb_ref[...]
    ccw = ccw_ref[...]
    gate_b = gb_ref[...]

    r64 = lax.broadcasted_iota(jnp.int32, (CHUNK, CHUNK), 0)
    c64 = lax.broadcasted_iota(jnp.int32, (CHUNK, CHUNK), 1)
    tri64 = jnp.where(c64 <= r64, 1.0, 0.0).astype(BF16)
    triu64 = jnp.where(r64 <= c64, 1.0, 0.0).astype(BF16)
    incl64 = c64 <= r64
    strict64 = c64 < r64
    r32 = lax.broadcasted_iota(jnp.int32, (SUB, SUB), 0)
    c32 = lax.broadcasted_iota(jnp.int32, (SUB, SUB), 1)
    tri32 = jnp.where(c32 <= r32, 1.0, 0.0).astype(BF16)
    incl32 = c32 <= r32
    rows_b = lax.broadcasted_iota(jnp.int32, (CHUNK, B_WIDTH), 0)

    def chunk_body(c, carry):
        r0 = pl.multiple_of(c * CHUNK, CHUNK)

        @pl.when(is_sample)
        def _():
            sa_s[...] = sh_ref[c]
            sc_s[...] = sg_ref[c]
            hb_s[...] = sr_ref[c]
            cvb_s[5:8, :] = src_ref[c]
            cvc_s[5:8, :] = sgc_ref[c]

        for sub in range(CHUNK // SUB):
            rs = pl.multiple_of(r0 + sub * SUB, SUB)
            zq = z_s[pl.ds(rs, SUB), Z_A:Z_A + HW]
            zf = z_s[pl.ds(rs, SUB), Z_A + HW:Z_A + 2 * HW]
            v = z_s[pl.ds(rs, SUB), Z_A + 2 * HW:Z_A + 3 * HW]
            zg = z_s[pl.ds(rs, SUB), Z_A + 3 * HW:Z_A + 4 * HW]
            q = _silu(zq)
            logf = _logaddexp(log_lb, log_1m_lb + _log_sigmoid(zf))
            k = one_m_lb * _sigmoid(-zf)
            b = _cumsum_rows(tri32, logf)
            b_last = b[SUB - 1:SUB, :]
            qe = (q * jnp.exp(b)).astype(BF16)
            ke = (k * jnp.exp(jnp.minimum(-b, 80.0))).astype(BF16)
            kd = (k * jnp.exp(b_last - b)).astype(BF16)
            eb_last = jnp.exp(b_last)
            vb = v.astype(BF16)
            outs = []
            for hd in range(N_HEADS):
                sl = slice(hd * D_HEAD, (hd + 1) * D_HEAD)
                att = jnp.where(incl32, _dot_nt(qe[:, sl], ke[:, sl]), 0.0)
                st = sa_s[hd]
                o = _dot(att.astype(BF16), vb[:, sl]) + _dot_nt(qe[:, sl], st.astype(BF16))
                sa_s[hd] = st * eb_last[:, sl] + _dot_tn(vb[:, sl], kd[:, sl])
                outs.append(_rms(o, a_norm[:, sl]))
            oa = jnp.concatenate(outs, axis=1) * _silu(zg)
            o_s[pl.ds(rs, SUB), 0:HW] = oa.astype(BF16)

        zx = z_s[pl.ds(r0, CHUNK), Z_B:Z_B + B_WIDTH]
        zbg = z_s[pl.ds(r0, CHUNK), Z_B + B_WIDTH:Z_B + 2 * B_WIDTH]
        xc = _causal_conv(cvb_s, zx, bcw) + bcb
        gates = _dot(xc.astype(BF16), wg_ref[...]) + gate_b
        r_g = _sigmoid(gates[:, 0:B_WIDTH])
        i_g = _sigmoid(gates[:, B_WIDTH:2 * B_WIDTH])
        log_a = -RG_C * r_g * sp_lam
        a = jnp.exp(log_a)
        mult = jnp.sqrt(-jnp.tanh(log_a) * (a * a + 1.0))
        first_off = jnp.where(jnp.logical_and(i == 0, c == 0), 0, 1 << 20)
        mult = jnp.where(rows_b + first_off == 0, 1.0, mult)
        u = mult * i_g * xc
        a_run, h_loc = _linear_scan(a, u, rows_b)
        hseq = h_loc + a_run * hb_s[...]
        hb_s[...] = hseq[CHUNK - 1:CHUNK, :]
        o_s[pl.ds(r0, CHUNK), HW:2 * HW] = (_gelu_tanh(zbg) * hseq).astype(BF16)

        zqkv = z_s[pl.ds(r0, CHUNK), Z_C:Z_C + QKV]
        zcg = z_s[pl.ds(r0, CHUNK), Z_C + QKV:Z_C + QKV + HW]
        zba = z_s[pl.ds(r0, CHUNK), Z_BA:Z_BA + 128]
        qkv = _silu(_causal_conv(cvc_s, zqkv, ccw))
        beta_all = _sigmoid(zba)
        g_all = neg_a * _softplus(zba + dtb)
        gc_col = _cumsum_rows(tri64, g_all)
        gc_row = _cumsum_cols(jnp.transpose(g_all), triu64)
        outs = []
        for hd in range(N_HEADS):
            sl = slice(hd * D_HEAD, (hd + 1) * D_HEAD)
            qh = qkv[:, hd * D_HEAD:(hd + 1) * D_HEAD]
            kh = qkv[:, HW + hd * D_HEAD:HW + (hd + 1) * D_HEAD]
            vh = qkv[:, 2 * HW + hd * D_HEAD:2 * HW + (hd + 1) * D_HEAD]
            qh = qh * lax.rsqrt(jnp.sum(qh * qh, axis=-1, keepdims=True) + EPS) * (D_HEAD ** -0.5)
            kh = kh * lax.rsqrt(jnp.sum(kh * kh, axis=-1, keepdims=True) + EPS)
            beta = beta_all[:, hd:hd + 1]
            gcc = gc_col[:, 4 + hd:5 + hd]
            gcr = gc_row[4 + hd:5 + hd, :]
            dec = jnp.where(incl64, jnp.exp(jnp.minimum(gcc - gcr, 0.0)), 0.0)
            kb = kh * beta
            khb = kh.astype(BF16)
            a_mat = jnp.where(strict64, _dot_nt(kb.astype(BF16), khb) * dec, 0.0)
            t_mat = _unit_lower_inverse(a_mat, r64, c64)
            egc = jnp.exp(gcc)
            rhs = jnp.concatenate([vh * beta, kb * egc], axis=1)
            uw = _dot(t_mat.astype(BF16), rhs.astype(BF16))
            s_old = sc_s[hd]
            lhs = jnp.concatenate([uw[:, D_HEAD:], qh * egc], axis=0)
            ws = _dot(lhs.astype(BF16), s_old.astype(BF16))
            v_new = uw[:, :D_HEAD] - ws[:CHUNK]
            qk = jnp.where(incl64, _dot_nt(qh.astype(BF16), khb) * dec, 0.0)
            o = ws[CHUNK:] + _dot(qk.astype(BF16), v_new.astype(BF16))
            g_last = gcc[CHUNK - 1:CHUNK, :]
            kdec = kh * jnp.exp(g_last - gcc)
            sc_s[hd] = s_old * jnp.exp(g_last) + _dot_tn(kdec.astype(BF16), v_new.astype(BF16))
            outs.append(_rms(o, c_norm[:, sl]))
        oc = jnp.concatenate(outs, axis=1) * _silu(zcg)
        o_s[pl.ds(r0, CHUNK), 2 * HW:3 * HW] = oc.astype(BF16)

        @pl.when(is_sample)
        def _():
            oh_ref[c] = sa_s[...]
            og_ref[c] = sc_s[...]
            or_ref[c] = hb_s[...]
            orc_ref[c] = cvb_s[5:8, :]
            ogc_ref[c] = cvc_s[5:8, :]

        return carry

    lax.fori_loop(0, tile // CHUNK, chunk_body, 0)

    @pl.when(i == n_ptiles - 1)
    def _():
        ph_ref[...] = sa_s[...]
        pg_ref[...] = sc_s[...]
        pr_ref[...] = hb_s[...]
        prc_ref[...] = cvb_s[5:8, :]
        pgc_ref[...] = cvc_s[5:8, :]

    merged = None
    for br in range(3):
        g = _sigmoid(z_s[:, Z_MERGE + br * D_MODEL:Z_MERGE + (br + 1) * D_MODEL])
        t = g * _dot(o_s[:, br * HW:(br + 1) * HW], wbr_ref[br])
        merged = t if merged is None else merged + t
    y = _dot(merged.astype(BF16), wout_ref[...])
    xo_ref[...] = x + _rms(y, npost_ref[...])


def _mlp_kernel(x_ref, g1_ref, wup_ref, wdn_ref, g2_ref, o_ref):
    x = x_ref[...]
    h = _rms(x, g1_ref[...]).astype(BF16)
    up = _dot(h, wup_ref[...])
    act = jnp.square(jnp.maximum(up, 0.0)).astype(BF16)
    m = _dot(act, wdn_ref[...])
    o_ref[...] = x + _rms(m, g2_ref[...])


def _resident(shape):
    nd = len(shape)
    return pl.BlockSpec(shape, lambda i, _nd=nd: (0,) * _nd, pipeline_mode=pl.Buffered(1))


def _mlp_call(x, g1, wup, wdn, g2, tile):
    n = x.shape[0]
    assert n % tile == 0
    return pl.pallas_call(
        _mlp_kernel,
        grid=(n // tile,),
        in_specs=[pl.BlockSpec((tile, D_MODEL), lambda i: (i, 0)),
                  _resident(g1.shape), _resident(wup.shape), _resident(wdn.shape), _resident(g2.shape)],
        out_specs=pl.BlockSpec((tile, D_MODEL), lambda i: (i, 0)),
        out_shape=jax.ShapeDtypeStruct(x.shape, x.dtype),
        compiler_params=pltpu.CompilerParams(dimension_semantics=("arbitrary",), vmem_limit_bytes=VMEM_LIMIT),
        name="mlp",
    )(x, g1, wup, wdn, g2)


def _mixer_call(x, states, weights, smalls, n_prompt, tile):
    n = x.shape[0]
    n_sample = (n - n_prompt) // CHUNK
    assert n_prompt % tile == 0 and (n - n_prompt) % tile == 0 and tile % CHUNK == 0
    n_ptiles = n_prompt // tile
    ts = tile // CHUNK
    sh, sr, src, sg, sgc = states

    def samp(shape):
        blk = (ts,) + shape[1:]
        nd = len(shape)
        return pl.BlockSpec(blk, lambda i, _nd=nd: (jnp.maximum(i - n_ptiles, 0),) + (0,) * (_nd - 1))

    def full(shape):
        nd = len(shape)
        return pl.BlockSpec(shape, lambda i, _nd=nd: (0,) * _nd)

    x_spec = pl.BlockSpec((tile, D_MODEL), lambda i: (i, 0))
    in_specs = ([x_spec] + [samp(s.shape) for s in states]
                + [_resident(w.shape) for w in weights] + [_resident(s.shape) for s in smalls])
    p_shapes = [(N_HEADS, D_HEAD, D_HEAD), (1, B_WIDTH), (CONV_W - 1, B_WIDTH),
                (N_HEADS, D_HEAD, D_HEAD), (CONV_W - 1, QKV)]
    out_shape = ([jax.ShapeDtypeStruct(x.shape, x.dtype)]
                 + [jax.ShapeDtypeStruct(s, F32) for s in p_shapes]
                 + [jax.ShapeDtypeStruct(s.shape, F32) for s in states])
    out_specs = [x_spec] + [full(s) for s in p_shapes] + [samp(s.shape) for s in states]
    scratch = [pltpu.VMEM((tile, Z_TOT), F32),
               pltpu.VMEM((tile, 3 * HW), BF16),
               pltpu.VMEM((N_HEADS, D_HEAD, D_HEAD), F32),
               pltpu.VMEM((N_HEADS, D_HEAD, D_HEAD), F32),
               pltpu.VMEM((1, B_WIDTH), F32),
               pltpu.VMEM((8 + CHUNK, B_WIDTH), F32),
               pltpu.VMEM((8 + CHUNK, QKV), F32)]
    del n_sample
    return pl.pallas_call(
        functools.partial(_mixer_kernel, n_ptiles),
        grid=(n // tile,),
        in_specs=in_specs,
        out_specs=out_specs,
        out_shape=out_shape,
        scratch_shapes=scratch,
        compiler_params=pltpu.CompilerParams(dimension_semantics=("arbitrary",), vmem_limit_bytes=VMEM_LIMIT),
        name="mixer",
    )(x, *states, *weights, *smalls)


def _prep_weights(w_in, b_gate_a_w, b_gate_x_w, w_br_a, w_br_b, w_br_c, w_out, w_up, w_down):
    depth = w_in.shape[0]
    pad = jnp.zeros((depth, D_MODEL, Z_MERGE - Z_BA - 8), w_in.dtype)
    win = jnp.concatenate([w_in[:, :, :Z_BA], w_in[:, :, Z_BA:Z_BA + 8], pad, w_in[:, :, Z_BA + 8:]],
                          axis=2).astype(BF16)
    nb = b_gate_a_w.shape[1]
    eye = jnp.eye(nb, dtype=b_gate_a_w.dtype)

    def dense(w):
        return jnp.einsum('nm,lncd->lncmd', eye, w).reshape(depth, B_WIDTH, B_WIDTH)

    wg = jnp.concatenate([dense(b_gate_a_w), dense(b_gate_x_w)], axis=2).astype(BF16)
    wbr = jnp.stack([w_br_a, w_br_b, w_br_c], axis=1).astype(BF16)
    return win, wbr, w_out.astype(BF16), wg, w_up.astype(BF16), w_down.astype(BF16)


def _forward(x_prompt, x_sample, state_hgrn, state_rglru, state_rglru_conv, state_gdn, state_gdn_conv,
             lb_raw, norm_pre_mix, norm_post_mix, norm_pre_mlp, norm_post_mlp, w_in, a_norm,
             b_conv_w, b_conv_b, b_gate_a_w, b_gate_a_b, b_gate_x_w, b_gate_x_b, b_lambda,
             c_conv_w, c_a_log, c_dt_bias, c_norm, w_br_a, w_br_b, w_br_c, w_out, w_up, w_down,
             mix_tile, mlp_tile):
    depth = w_in.shape[0]
    bp, seq, _ = x_prompt.shape
    nsb, dec_seq, _ = x_sample.shape
    assert bp == 1 and dec_seq == CHUNK
    n_prompt = bp * seq

    lb_cum = jnp.cumsum(jax.nn.softmax(lb_raw.astype(F32), axis=0), axis=0)
    lower = lb_cum - lb_cum[0:1]

    win, wbr, wout, wg, wup, wdn = _prep_weights(w_in, b_gate_a_w, b_gate_x_w, w_br_a, w_br_b, w_br_c,
                                                 w_out, w_up, w_down)
    row = lambda v: v.reshape(1, -1).astype(F32)
    lane_pad = lambda v: jnp.zeros((1, 128), F32).at[0, 4:8].set(v.astype(F32))

    x = jnp.concatenate([x_prompt.reshape(n_prompt, D_MODEL), x_sample.reshape(nsb * dec_seq, D_MODEL)], axis=0)
    sh_t = jnp.swapaxes(state_hgrn, -1, -2)
    new_p, new_s = [], []
    for l in range(depth):
        states = (sh_t[l], state_rglru[l].reshape(nsb, 1, B_WIDTH), state_rglru_conv[l],
                  state_gdn[l], state_gdn_conv[l])
        weights = (win[l], wbr[l], wout[l], wg[l])
        smalls = (row(norm_pre_mix[l]), row(norm_post_mix[l]),
                  row(jnp.tile(a_norm[l], N_HEADS)), row(jnp.tile(c_norm[l], N_HEADS)),
                  row(lower[l]), b_conv_w[l].astype(F32), row(b_conv_b[l]),
                  row(jnp.concatenate([b_gate_a_b[l], b_gate_x_b[l]])),
                  row(b_lambda[l]), c_conv_w[l].astype(F32), lane_pad(c_a_log[l]), lane_pad(c_dt_bias[l]))
        outs = _mixer_call(x, states, weights, smalls, n_prompt, mix_tile)
        x = outs[0]
        new_p.append(outs[1:6])
        new_s.append(outs[6:11])
        x = _mlp_call(x, row(norm_pre_mlp[l]), wup[l], wdn[l], row(norm_post_mlp[l]), mlp_tile)

    yp = x[:n_prompt].reshape(bp, seq, D_MODEL)
    ys = x[n_prompt:].reshape(nsb, dec_seq, D_MODEL)
    stack = lambda items, j: jnp.stack([it[j] for it in items])
    sdt = state_hgrn.dtype
    p_hgrn = jnp.swapaxes(stack(new_p, 0), -1, -2)[:, None].astype(sdt)
    p_rglru = stack(new_p, 1).astype(sdt)
    p_rglru_conv = stack(new_p, 2)[:, None].astype(sdt)
    p_gdn = stack(new_p, 3)[:, None].astype(sdt)
    p_gdn_conv = stack(new_p, 4)[:, None].astype(sdt)
    s_hgrn = jnp.swapaxes(stack(new_s, 0), -1, -2).astype(sdt)
    s_rglru = stack(new_s, 1).reshape(depth, nsb, B_WIDTH).astype(sdt)
    s_rglru_conv = stack(new_s, 2).astype(sdt)
    s_gdn = stack(new_s, 3).astype(sdt)
    s_gdn_conv = stack(new_s, 4).astype(sdt)
    return (yp, ys, p_hgrn, p_rglru, p_rglru_conv, p_gdn, p_gdn_conv,
            s_hgrn, s_rglru, s_rglru_conv, s_gdn, s_gdn_conv)


def kernel(x_prompt, x_sample, state_hgrn, state_rglru, state_rglru_conv, state_gdn, state_gdn_conv, lb_raw, norm_pre_mix, norm_post_mix, norm_pre_mlp, norm_post_mlp, w_in, a_norm, b_conv_w, b_conv_b, b_gate_a_w, b_gate_a_b, b_gate_x_w, b_gate_x_b, b_lambda, c_conv_w, c_a_log, c_dt_bias, c_norm, w_br_a, w_br_b, w_br_c, w_out, w_up, w_down):
    return _forward(x_prompt, x_sample, state_hgrn, state_rglru, state_rglru_conv, state_gdn, state_gdn_conv,
                    lb_raw, norm_pre_mix, norm_post_mix, norm_pre_mlp, norm_post_mlp, w_in, a_norm,
                    b_conv_w, b_conv_b, b_gate_a_w, b_gate_a_b, b_gate_x_w, b_gate_x_b, b_lambda,
                    c_conv_w, c_a_log, c_dt_bias, c_norm, w_br_a, w_br_b, w_br_c, w_out, w_up, w_down,
                    MIX_TILE, MLP_TILE)
```

```python
import functools
import math

import jax
import jax.numpy as jnp
from jax import lax
from jax.experimental import pallas as pl
from jax.experimental.pallas import tpu as pltpu

F32 = jnp.float32
BF16 = jnp.bfloat16

D_MODEL = 1024
N_HEADS = 4
D_HEAD = 128
HW = N_HEADS * D_HEAD
B_WIDTH = 512
QKV = 3 * HW
D_FF = 4 * D_MODEL
CONV_W = 4
RG_C = 8.0
EPS = 1e-6
CHUNK = 64
SUB = 32
CONV_ROWS = 8 + CHUNK

Z_A = 0
Z_B = 2048
Z_C = 3072
Z_BA = 5120
Z_MERGE = 5248
Z_TOT = 8320
Z_GROUPS = ((0, 2048), (2048, 3072), (3072, 5248), (5248, 8320))

MIX_TILE = 256
MLP_TILE = 512
VMEM_LIMIT = 60 * 1024 * 1024


def _dot(a, b):
    return jnp.dot(a, b, preferred_element_type=F32)


def _dot_nt(a, b):
    return lax.dot_general(a, b, (((1,), (1,)), ((), ())), preferred_element_type=F32)


def _dot_tn(a, b):
    return lax.dot_general(a, b, (((0,), (0,)), ((), ())), preferred_element_type=F32)


def _sigmoid(x):
    return 1.0 / (1.0 + jnp.exp(-x))


def _silu(x):
    return x * _sigmoid(x)


def _softplus(x):
    return jnp.maximum(x, 0.0) + jnp.log1p(jnp.exp(-jnp.abs(x)))


def _log_sigmoid(x):
    return jnp.minimum(x, 0.0) - jnp.log1p(jnp.exp(-jnp.abs(x)))


def _logaddexp(a, b):
    return jnp.maximum(a, b) + jnp.log1p(jnp.exp(-jnp.abs(a - b)))


def _gelu_tanh(x):
    return 0.5 * x * (1.0 + jnp.tanh(math.sqrt(2.0 / math.pi) * (x + 0.044715 * (x * x * x))))


def _rms(x, w):
    return x * lax.rsqrt(jnp.mean(x * x, axis=-1, keepdims=True) + EPS) * w


def _split3(x):
    hi = x.astype(BF16)
    r = x - hi.astype(F32)
    mid = r.astype(BF16)
    lo = (r - mid.astype(F32)).astype(BF16)
    return hi, mid, lo


def _cumsum_rows(tri, x):
    hi, mid, lo = _split3(x)
    return _dot(tri, hi) + _dot(tri, mid) + _dot(tri, lo)


def _cumsum_cols(x, triu):
    hi, mid, lo = _split3(x)
    return _dot(hi, triu) + _dot(mid, triu) + _dot(lo, triu)


def _unit_lower_inverse(a, rowid, colid, block):
    eye = rowid == colid
    sib = (rowid ^ colid) == 1
    td = jnp.where(eye, 1.0, 0.0) - jnp.where(sib & (colid < rowid), a, 0.0)
    shift = 1
    while (1 << shift) < block:
        rb = rowid >> shift
        sib = ((rb & 1) == 1) & ((colid >> shift) == rb - 1)
        a_off = jnp.where(sib, a, 0.0)
        tdb = td.astype(BF16)
        x = _dot(tdb, a_off.astype(BF16))
        td = td - _dot(x.astype(BF16), tdb)
        shift += 1
    return td


def _linear_scan(a, u, rowid):
    n = a.shape[0]
    d = 1
    while d < n:
        a_s = pltpu.roll(a, d, 0)
        u_s = pltpu.roll(u, d, 0)
        m = rowid >= d
        u = jnp.where(m, u + a * u_s, u)
        a = jnp.where(m, a * a_s, a)
        d *= 2
    return a, u


def _mixer_kernel(is_sample, n_tiles, *refs):
    if is_sample:
        (x_ref, sh_ref, sr_ref, src_ref, sg_ref, sgc_ref) = refs[:6]
        refs = refs[6:]
    else:
        x_ref = refs[0]
        refs = refs[1:]
    (win_ref, wbr_ref, wout_ref, wg_ref,
     npre_ref, npost_ref, anorm_ref, cnorm_ref, lb_ref, bcw_ref, bcb_ref, gb_ref,
     lam_ref, ccw_ref, alog_ref, dtb_ref,
     xo_ref, oh_ref, or_ref, orc_ref, og_ref, ogc_ref,
     z_s, o_s, sa_s, sc_s, hb_s, cvb_s, cvc_s,
     qe_s, ke_s, kd_s, va_s, xcb_s, q_s, k_s, kb_s, qeg_s, kdec_s, rhs_s) = refs

    i = pl.program_id(0)
    tile = x_ref.shape[0]
    n_chunks = tile // CHUNK
    n_subs = tile // SUB

    if not is_sample:
        @pl.when(i == 0)
        def _():
            sa_s[...] = jnp.zeros_like(sa_s)
            sc_s[...] = jnp.zeros_like(sc_s)
            hb_s[...] = jnp.zeros_like(hb_s)
            cvb_s[0, 0:8, :] = jnp.zeros((8, B_WIDTH), F32)
            cvc_s[0, 0:8, :] = jnp.zeros((8, QKV), F32)

    x = x_ref[...]
    h = _rms(x, npre_ref[...]).astype(BF16)
    for c0, c1 in Z_GROUPS:
        z_s[:, c0:c1] = _dot(h, win_ref[:, c0:c1])

    lb = lb_ref[...]
    log_lb = jnp.log(lb)
    log_1m_lb = jnp.log1p(-lb)
    one_m_lb = 1.0 - lb
    sp_lam = _softplus(-lam_ref[...])
    neg_a = -jnp.exp(alog_ref[...])
    dtb = dtb_ref[...]
    a_norm = anorm_ref[...]
    c_norm = cnorm_ref[...]
    bcw = bcw_ref[...]
    bcb = bcb_ref[...]
    ccw = ccw_ref[...]
    gate_b = gb_ref[...]

    rt = lax.broadcasted_iota(jnp.int32, (tile, tile), 0)
    ct = lax.broadcasted_iota(jnp.int32, (tile, tile), 1)
    same64 = (rt >> 6) == (ct >> 6)
    incl64 = same64 & (ct <= rt)
    strict64 = same64 & (ct < rt)
    incl32 = ((rt >> 5) == (ct >> 5)) & (ct <= rt)
    tri64 = jnp.where(incl64, 1.0, 0.0).astype(BF16)
    triu64 = jnp.where(same64 & (rt <= ct), 1.0, 0.0).astype(BF16)
    r64 = lax.broadcasted_iota(jnp.int32, (CHUNK, CHUNK), 0)
    c64 = lax.broadcasted_iota(jnp.int32, (CHUNK, CHUNK), 1)
    tri32x2 = jnp.where(((r64 >> 5) == (c64 >> 5)) & (c64 <= r64), 1.0, 0.0).astype(BF16)
    rows_b = lax.broadcasted_iota(jnp.int32, (CHUNK, B_WIDTH), 0)

    def conv_chunk(buf_ref, c, xc_in, prev_tail, state_ref, w):
        buf_ref[c, 8:CONV_ROWS, :] = xc_in
        if is_sample:
            buf_ref[c, 5:8, :] = state_ref[c]
        elif c > 0:
            buf_ref[c, 5:8, :] = prev_tail
        y = xc_in * w[3:4, :]
        for j in range(CONV_W - 1):
            y = y + buf_ref[c, 5 + j:5 + j + CHUNK, :] * w[j:j + 1, :]
        return y

    eb_last = []
    for c in range(n_chunks):
        rows = slice(c * CHUNK, (c + 1) * CHUNK)
        zq = z_s[rows, Z_A:Z_A + HW]
        zf = z_s[rows, Z_A + HW:Z_A + 2 * HW]
        q = _silu(zq)
        logf = _logaddexp(log_lb, log_1m_lb + _log_sigmoid(zf))
        k = one_m_lb * _sigmoid(-zf)
        b = _cumsum_rows(tri32x2, logf)
        qe_s[rows, :] = (q * jnp.exp(b)).astype(BF16)
        ke_s[rows, :] = (k * jnp.exp(jnp.minimum(-b, 80.0))).astype(BF16)
        va_s[rows, :] = z_s[rows, Z_A + 2 * HW:Z_A + 3 * HW].astype(BF16)
        for sub in range(CHUNK // SUB):
            b_last = b[sub * SUB + SUB - 1:sub * SUB + SUB, :]
            bs = b[sub * SUB:(sub + 1) * SUB, :]
            ks = k[sub * SUB:(sub + 1) * SUB, :]
            kd_s[c * CHUNK + sub * SUB:c * CHUNK + (sub + 1) * SUB, :] = (ks * jnp.exp(b_last - bs)).astype(BF16)
            eb_last.append(jnp.exp(b_last))

    for hd in range(N_HEADS):
        sl = slice(hd * D_HEAD, (hd + 1) * D_HEAD)
        att = jnp.where(incl32, _dot_nt(qe_s[:, sl], ke_s[:, sl]), 0.0)
        intra = _dot(att.astype(BF16), va_s[:, sl])
        st = None if is_sample else sa_s[hd]
        for j in range(n_subs):
            rj = slice(j * SUB, (j + 1) * SUB)
            if is_sample and j % 2 == 0:
                st = sh_ref[j // 2, hd]
            o = intra[rj, :] + _dot_nt(qe_s[rj, sl], st.astype(BF16))
            st = st * eb_last[j][:, sl] + _dot_tn(va_s[rj, sl], kd_s[rj, sl])
            if is_sample and j % 2 == 1:
                oh_ref[j // 2, hd] = st
            zg = z_s[rj, Z_A + 3 * HW + hd * D_HEAD:Z_A + 3 * HW + (hd + 1) * D_HEAD]
            o_s[rj, sl] = (_rms(o, a_norm[:, sl]) * _silu(zg)).astype(BF16)
        if not is_sample:
            sa_s[hd] = st

    prev_tail = None
    xcs = []
    for c in range(n_chunks):
        rows = slice(c * CHUNK, (c + 1) * CHUNK)
        zx = z_s[rows, Z_B:Z_B + B_WIDTH]
        xc = conv_chunk(cvb_s, c, zx, prev_tail, src_ref if is_sample else None, bcw) + bcb
        prev_tail = zx[CHUNK - 3:CHUNK, :]
        if is_sample:
            orc_ref[c] = prev_tail
        xcb_s[rows, :] = xc.astype(BF16)
        xcs.append(xc)
    if not is_sample:
        cvb_s[0, 5:8, :] = prev_tail
    gates = _dot(xcb_s[...], wg_ref[...]) + gate_b
    h_prev = None if is_sample else hb_s[...]
    for c in range(n_chunks):
        rows = slice(c * CHUNK, (c + 1) * CHUNK)
        xc = xcs[c]
        r_g = _sigmoid(gates[rows, 0:B_WIDTH])
        i_g = _sigmoid(gates[rows, B_WIDTH:2 * B_WIDTH])
        log_a = -RG_C * r_g * sp_lam
        a = jnp.exp(log_a)
        mult = jnp.sqrt(-jnp.tanh(log_a) * (a * a + 1.0))
        if not is_sample and c == 0:
            first_off = jnp.where(i == 0, 0, 1 << 20)
            mult = jnp.where(rows_b + first_off == 0, 1.0, mult)
        a_run, h_loc = _linear_scan(a, mult * i_g * xc, rows_b)
        if is_sample:
            h_prev = sr_ref[c]
        hseq = h_loc + a_run * h_prev
        h_prev = hseq[CHUNK - 1:CHUNK, :]
        if is_sample:
            or_ref[c] = h_prev
        zbg = z_s[rows, Z_B + B_WIDTH:Z_B + 2 * B_WIDTH]
        o_s[rows, HW:2 * HW] = (_gelu_tanh(zbg) * hseq).astype(BF16)
    if not is_sample:
        hb_s[...] = h_prev

    zba = z_s[:, Z_BA:Z_BA + 128]
    beta_all = _sigmoid(zba)
    g_all = neg_a * _softplus(zba + dtb)
    gc_col = _cumsum_rows(tri64, g_all)
    gc_row = _cumsum_cols(jnp.transpose(g_all), triu64)
    g_last = [gc_col[c * CHUNK + CHUNK - 1:(c + 1) * CHUNK, :] for c in range(n_chunks)]
    g_last_rows = jnp.concatenate([jnp.broadcast_to(g, (CHUNK, 128)) for g in g_last], axis=0)
    e_gc = jnp.exp(gc_col)
    e_gld = jnp.exp(g_last_rows - gc_col)

    prev_tail = None
    for c in range(n_chunks):
        rows = slice(c * CHUNK, (c + 1) * CHUNK)
        zqkv = z_s[rows, Z_C:Z_C + QKV]
        qkv = _silu(conv_chunk(cvc_s, c, zqkv, prev_tail, sgc_ref if is_sample else None, ccw))
        prev_tail = zqkv[CHUNK - 3:CHUNK, :]
        if is_sample:
            ogc_ref[c] = prev_tail
        for hd in range(N_HEADS):
            sl = slice(hd * D_HEAD, (hd + 1) * D_HEAD)
            qh = qkv[:, hd * D_HEAD:(hd + 1) * D_HEAD]
            kh = qkv[:, HW + hd * D_HEAD:HW + (hd + 1) * D_HEAD]
            vh = qkv[:, 2 * HW + hd * D_HEAD:2 * HW + (hd + 1) * D_HEAD]
            qh = qh * lax.rsqrt(jnp.sum(qh * qh, axis=-1, keepdims=True) + EPS) * (D_HEAD ** -0.5)
            kh = kh * lax.rsqrt(jnp.sum(kh * kh, axis=-1, keepdims=True) + EPS)
            beta = beta_all[rows, hd:hd + 1]
            egc = e_gc[rows, 4 + hd:5 + hd]
            kb = kh * beta
            q_s[rows, sl] = qh.astype(BF16)
            k_s[rows, sl] = kh.astype(BF16)
            kb_s[rows, sl] = kb.astype(BF16)
            qeg_s[rows, sl] = (qh * egc).astype(BF16)
            kdec_s[rows, sl] = (kh * e_gld[rows, 4 + hd:5 + hd]).astype(BF16)
            rhs_s[rows, 2 * hd * D_HEAD:(2 * hd + 1) * D_HEAD] = (vh * beta).astype(BF16)
            rhs_s[rows, (2 * hd + 1) * D_HEAD:(2 * hd + 2) * D_HEAD] = (kb * egc).astype(BF16)
    if not is_sample:
        cvc_s[0, 5:8, :] = prev_tail

    for hd in range(N_HEADS):
        sl = slice(hd * D_HEAD, (hd + 1) * D_HEAD)
        gcc = gc_col[:, 4 + hd:5 + hd]
        gcr = gc_row[4 + hd:5 + hd, :]
        dec = jnp.exp(jnp.minimum(gcc - gcr, 0.0))
        a_mat = jnp.where(strict64, _dot_nt(kb_s[:, sl], k_s[:, sl]) * dec, 0.0)
        t_mat = _unit_lower_inverse(a_mat, rt, ct, CHUNK)
        uw = _dot(t_mat.astype(BF16), rhs_s[:, 2 * hd * D_HEAD:(2 * hd + 2) * D_HEAD])
        qk = jnp.where(incl64, _dot_nt(q_s[:, sl], k_s[:, sl]) * dec, 0.0)
        s_cur = None if is_sample else sc_s[hd]
        for c in range(n_chunks):
            rows = slice(c * CHUNK, (c + 1) * CHUNK)
            if is_sample:
                s_cur = sg_ref[c, hd]
            lhs = jnp.concatenate([uw[rows, D_HEAD:].astype(BF16), qeg_s[rows, sl]], axis=0)
            ws = _dot(lhs, s_cur.astype(BF16))
            v_new = (uw[rows, :D_HEAD] - ws[:CHUNK]).astype(BF16)
            o = ws[CHUNK:] + _dot(qk[rows, c * CHUNK:(c + 1) * CHUNK].astype(BF16), v_new)
            s_cur = s_cur * jnp.exp(g_last[c][:, 4 + hd:5 + hd]) + _dot_tn(kdec_s[rows, sl], v_new)
            if is_sample:
                og_ref[c, hd] = s_cur
            zcg = z_s[rows, Z_C + QKV + hd * D_HEAD:Z_C + QKV + (hd + 1) * D_HEAD]
            o_s[rows, 2 * HW + hd * D_HEAD:2 * HW + (hd + 1) * D_HEAD] = (
                _rms(o, c_norm[:, sl]) * _silu(zcg)).astype(BF16)
        if not is_sample:
            sc_s[hd] = s_cur

    if not is_sample:
        @pl.when(i == n_tiles - 1)
        def _():
            oh_ref[...] = sa_s[...]
            og_ref[...] = sc_s[...]
            or_ref[...] = hb_s[...]
            orc_ref[...] = cvb_s[0, 5:8, :]
            ogc_ref[...] = cvc_s[0, 5:8, :]

    merged = None
    for br in range(3):
        g = _sigmoid(z_s[:, Z_MERGE + br * D_MODEL:Z_MERGE + (br + 1) * D_MODEL])
        t = g * _dot(o_s[:, br * HW:(br + 1) * HW], wbr_ref[br])
        merged = t if merged is None else merged + t
    y = _dot(merged.astype(BF16), wout_ref[...])
    xo_ref[...] = x + _rms(y, npost_ref[...])


def _mlp_kernel(x_ref, g1_ref, wup_ref, wdn_ref, g2_ref, o_ref):
    x = x_ref[...]
    h = _rms(x, g1_ref[...]).astype(BF16)
    up = _dot(h, wup_ref[...])
    act = jnp.square(jnp.maximum(up, 0.0)).astype(BF16)
    m = _dot(act, wdn_ref[...])
    o_ref[...] = x + _rms(m, g2_ref[...])


def _resident(shape):
    nd = len(shape)
    return pl.BlockSpec(shape, lambda i, _nd=nd: (0,) * _nd, pipeline_mode=pl.Buffered(1))


def _mlp_call(x, g1, wup, wdn, g2, tile):
    n = x.shape[0]
    assert n % tile == 0
    return pl.pallas_call(
        _mlp_kernel,
        grid=(n // tile,),
        in_specs=[pl.BlockSpec((tile, D_MODEL), lambda i: (i, 0)),
                  _resident(g1.shape), _resident(wup.shape), _resident(wdn.shape), _resident(g2.shape)],
        out_specs=pl.BlockSpec((tile, D_MODEL), lambda i: (i, 0)),
        out_shape=jax.ShapeDtypeStruct(x.shape, x.dtype),
        compiler_params=pltpu.CompilerParams(dimension_semantics=("arbitrary",), vmem_limit_bytes=VMEM_LIMIT),
        name="mlp",
    )(x, g1, wup, wdn, g2)


def _mixer_call(x, states, weights, smalls, tile):
    n = x.shape[0]
    assert n % tile == 0 and tile % CHUNK == 0
    n_tiles = n // tile
    ts = tile // CHUNK
    is_sample = states is not None

    def full(shape):
        nd = len(shape)
        return pl.BlockSpec(shape, lambda i, _nd=nd: (0,) * _nd)

    def per_stream(shape):
        nd = len(shape)
        return pl.BlockSpec((ts,) + shape[1:], lambda i, _nd=nd: (i,) + (0,) * (_nd - 1))

    x_spec = pl.BlockSpec((tile, D_MODEL), lambda i: (i, 0))
    in_specs = [x_spec]
    if is_sample:
        in_specs += [per_stream(s.shape) for s in states]
        st_shapes = [s.shape for s in states]
        st_specs = [per_stream(s) for s in st_shapes]
    else:
        st_shapes = [(N_HEADS, D_HEAD, D_HEAD), (1, B_WIDTH), (CONV_W - 1, B_WIDTH),
                     (N_HEADS, D_HEAD, D_HEAD), (CONV_W - 1, QKV)]
        st_specs = [full(s) for s in st_shapes]
    in_specs += [_resident(w.shape) for w in weights] + [_resident(s.shape) for s in smalls]
    out_shape = [jax.ShapeDtypeStruct(x.shape, x.dtype)] + [jax.ShapeDtypeStruct(s, F32) for s in st_shapes]
    out_specs = [x_spec] + st_specs
    act = lambda w: pltpu.VMEM((tile, w), BF16)
    scratch = [pltpu.VMEM((tile, Z_TOT), F32),
               act(3 * HW),
               pltpu.VMEM((N_HEADS, D_HEAD, D_HEAD), F32),
               pltpu.VMEM((N_HEADS, D_HEAD, D_HEAD), F32),
               pltpu.VMEM((1, B_WIDTH), F32),
               pltpu.VMEM((ts, CONV_ROWS, B_WIDTH), F32),
               pltpu.VMEM((ts, CONV_ROWS, QKV), F32),
               act(HW), act(HW), act(HW), act(HW),
               act(B_WIDTH),
               act(HW), act(HW), act(HW), act(HW), act(HW),
               act(2 * HW)]
    args = (x,) + (tuple(states) if is_sample else ()) + tuple(weights) + tuple(smalls)
    return pl.pallas_call(
        functools.partial(_mixer_kernel, is_sample, n_tiles),
        grid=(n_tiles,),
        in_specs=in_specs,
        out_specs=out_specs,
        out_shape=out_shape,
        scratch_shapes=scratch,
        compiler_params=pltpu.CompilerParams(dimension_semantics=("arbitrary",), vmem_limit_bytes=VMEM_LIMIT),
        name="mixer_sample" if is_sample else "mixer_prompt",
    )(*args)


def _prep_weights(w_in, b_gate_a_w, b_gate_x_w, w_br_a, w_br_b, w_br_c, w_out, w_up, w_down):
    depth = w_in.shape[0]
    pad = jnp.zeros((depth, D_MODEL, Z_MERGE - Z_BA - 8), w_in.dtype)
    win = jnp.concatenate([w_in[:, :, :Z_BA], w_in[:, :, Z_BA:Z_BA + 8], pad, w_in[:, :, Z_BA + 8:]],
                          axis=2).astype(BF16)
    nb = b_gate_a_w.shape[1]
    eye = jnp.eye(nb, dtype=b_gate_a_w.dtype)

    def dense(w):
        return jnp.einsum('nm,lncd->lncmd', eye, w).reshape(depth, B_WIDTH, B_WIDTH)

    wg = jnp.concatenate([dense(b_gate_a_w), dense(b_gate_x_w)], axis=2).astype(BF16)
    wbr = jnp.stack([w_br_a, w_br_b, w_br_c], axis=1).astype(BF16)
    return win, wbr, w_out.astype(BF16), wg, w_up.astype(BF16), w_down.astype(BF16)


def _forward(x_prompt, x_sample, state_hgrn, state_rglru, state_rglru_conv, state_gdn, state_gdn_conv,
             lb_raw, norm_pre_mix, norm_post_mix, norm_pre_mlp, norm_post_mlp, w_in, a_norm,
             b_conv_w, b_conv_b, b_gate_a_w, b_gate_a_b, b_gate_x_w, b_gate_x_b, b_lambda,
             c_conv_w, c_a_log, c_dt_bias, c_norm, w_br_a, w_br_b, w_br_c, w_out, w_up, w_down,
             mix_tile, mlp_tile):
    depth = w_in.shape[0]
    bp, seq, _ = x_prompt.shape
    nsb, dec_seq, _ = x_sample.shape
    assert bp == 1 and dec_seq == CHUNK

    lb_cum = jnp.cumsum(jax.nn.softmax(lb_raw.astype(F32), axis=0), axis=0)
    lower = lb_cum - lb_cum[0:1]

    win, wbr, wout, wg, wup, wdn = _prep_weights(w_in, b_gate_a_w, b_gate_x_w, w_br_a, w_br_b, w_br_c,
                                                 w_out, w_up, w_down)
    row = lambda v: v.reshape(1, -1).astype(F32)
    lane_pad = lambda v: jnp.zeros((1, 128), F32).at[0, 4:8].set(v.astype(F32))

    xp = x_prompt.reshape(seq, D_MODEL)
    xs = x_sample.reshape(nsb * dec_seq, D_MODEL)
    sh_t = jnp.swapaxes(state_hgrn, -1, -2)
    new_p, new_s = [], []
    for l in range(depth):
        states = (sh_t[l], state_rglru[l].reshape(nsb, 1, B_WIDTH), state_rglru_conv[l],
                  state_gdn[l], state_gdn_conv[l])
        weights = (win[l], wbr[l], wout[l], wg[l])
        smalls = (row(norm_pre_mix[l]), row(norm_post_mix[l]),
                  row(jnp.tile(a_norm[l], N_HEADS)), row(jnp.tile(c_norm[l], N_HEADS)),
                  row(lower[l]), b_conv_w[l].astype(F32), row(b_conv_b[l]),
                  row(jnp.concatenate([b_gate_a_b[l], b_gate_x_b[l]])),
                  row(b_lambda[l]), c_conv_w[l].astype(F32), lane_pad(c_a_log[l]), lane_pad(c_dt_bias[l]))
        g1, g2 = row(norm_pre_mlp[l]), row(norm_post_mlp[l])
        outs = _mixer_call(xp, None, weights, smalls, mix_tile)
        new_p.append(outs[1:])
        xp = _mlp_call(outs[0], g1, wup[l], wdn[l], g2, mlp_tile)
        outs = _mixer_call(xs, states, weights, smalls, mix_tile)
        new_s.append(outs[1:])
        xs = _mlp_call(outs[0], g1, wup[l], wdn[l], g2, mlp_tile)

    yp = xp.reshape(bp, seq, D_MODEL)
    ys = xs.reshape(nsb, dec_seq, D_MODEL)
    stack = lambda items, j: jnp.stack([it[j] for it in items])
    sdt = state_hgrn.dtype
    p_hgrn = jnp.swapaxes(stack(new_p, 0), -1, -2)[:, None].astype(sdt)
    p_rglru = stack(new_p, 1).astype(sdt)
    p_rglru_conv = stack(new_p, 2)[:, None].astype(sdt)
    p_gdn = stack(new_p, 3)[:, None].astype(sdt)
    p_gdn_conv = stack(new_p, 4)[:, None].astype(sdt)
    s_hgrn = jnp.swapaxes(stack(new_s, 0), -1, -2).astype(sdt)
    s_rglru = stack(new_s, 1).reshape(depth, nsb, B_WIDTH).astype(sdt)
    s_rglru_conv = stack(new_s, 2).astype(sdt)
    s_gdn = stack(new_s, 3).astype(sdt)
    s_gdn_conv = stack(new_s, 4).astype(sdt)
    return (yp, ys, p_hgrn, p_rglru, p_rglru_conv, p_gdn, p_gdn_conv,
            s_hgrn, s_rglru, s_rglru_conv, s_gdn, s_gdn_conv)


def kernel(x_prompt, x_sample, state_hgrn, state_rglru, state_rglru_conv, state_gdn, state_gdn_conv, lb_raw, norm_pre_mix, norm_post_mix, norm_pre_mlp, norm_post_mlp, w_in, a_norm, b_conv_w, b_conv_b, b_gate_a_w, b_gate_a_b, b_gate_x_w, b_gate_x_b, b_lambda, c_conv_w, c_a_log, c_dt_bias, c_norm, w_br_a, w_br_b, w_br_c, w_out, w_up, w_down):
    return _forward(x_prompt, x_sample, state_hgrn, state_rglru, state_rglru_conv, state_gdn, state_gdn_conv,
                    lb_raw, norm_pre_mix, norm_post_mix, norm_pre_mlp, norm_post_mlp, w_in, a_norm,
                    b_conv_w, b_conv_b, b_gate_a_w, b_gate_a_b, b_gate_x_w, b_gate_x_b, b_lambda,
                    c_conv_w, c_a_log, c_dt_bias, c_norm, w_br_a, w_br_b, w_br_c, w_out, w_up, w_down,
                    MIX_TILE, MLP_TILE)
```
